```python
import math, functools
import jax, jax.numpy as jnp
from jax import lax
import numpy as np

D_MODEL = 1024
BATCH = 16
SEQ = 2048
DEPTH = 1
DEC_BATCH = 32
DEC_SEQ = 4
PAST_LEN = 16384
PAGE_SIZE = 128

D_BRANCH = D_MODEL
SSM_HEAD_DIM = 64
SSM_HEADS = D_BRANCH // SSM_HEAD_DIM
SSM_GROUPS = 2
D_STATE = 128
CONV_WIDTH = 4
CONV_DIM = D_BRANCH + 2 * SSM_GROUPS * D_STATE
SSD_CHUNK = 128
ATTN_HEAD_DIM = 64
ATTN_HEADS = D_BRANCH // ATTN_HEAD_DIM
MOBA_BLOCK = 256
MOBA_TOPK = 3
Q_CHUNK = 32
NORM_EPS = 1e-6
OFF_XBC = D_BRANCH
OFF_DT = OFF_XBC + CONV_DIM
OFF_Q = OFF_DT + SSM_HEADS
OFF_K = OFF_Q + D_BRANCH
OFF_V = OFF_K + D_BRANCH
OFF_ZB = OFF_V + D_BRANCH
OFF_GA = OFF_ZB + D_BRANCH
OFF_GB = OFF_GA + D_MODEL
D_IN = OFF_GB + D_MODEL
SPLIT_POINTS = (OFF_XBC, OFF_DT, OFF_Q, OFF_K, OFF_V, OFF_ZB, OFF_GA, OFF_GB)

kernel_name = "hybrid_ssd_moba_gated_merge_step"


def _rmsnorm(x, g):
    xf = x.astype(jnp.float32)
    y = xf * lax.rsqrt(jnp.mean(xf * xf, axis=-1, keepdims=True) + NORM_EPS)
    return (y * g.astype(jnp.float32)).astype(x.dtype)


def _causal_conv(u, buf, w, b):
    L = u.shape[1]
    up = jnp.concatenate([buf.astype(u.dtype), u], axis=1)
    acc = b + w[0] * up[:, 0:L]
    for i in range(1, CONV_WIDTH):
        acc = acc + w[i] * up[:, i:i + L]
    return jax.nn.silu(acc), up[:, L:]


def _ssd_scan(x, dt, a, bm, cm, h0):
    b, L, H, P = x.shape
    G, N = bm.shape[2], bm.shape[3]
    R = H // G
    Q = SSD_CHUNK if L % SSD_CHUNK == 0 else L
    nc = L // Q
    f32 = jnp.float32
    xc = x.astype(f32).reshape(b, nc, Q, G, R, P)
    dtc = dt.reshape(b, nc, Q, G, R)
    bc = bm.astype(f32).reshape(b, nc, Q, G, N)
    cc = cm.astype(f32).reshape(b, nc, Q, G, N)
    acum = jnp.cumsum(dtc * a.reshape(G, R), axis=2)
    causal = jnp.tril(jnp.ones((Q, Q), bool))
    seg = acum[:, :, :, None] - acum[:, :, None]
    decay = jnp.exp(jnp.where(causal[:, :, None, None], seg, -jnp.inf))
    cb = jnp.einsum('bcign,bcjgn->bcijg', cc, bc)
    w_ij = cb[..., None] * decay * dtc[:, :, None]
    y_diag = jnp.einsum('bcijgr,bcjgrp->bcigrp', w_ij, xc)
    decay_end = jnp.exp(acum[:, :, -1:] - acum) * dtc
    states = jnp.einsum('bcjgrp,bcjgn->bcgrpn', xc * decay_end[..., None], bc)
    chunk_decay = jnp.exp(acum[:, :, -1])

    def step(h, inp):
        s, d = inp
        return d[..., None, None] * h + s, h

    h_last, h_prev = lax.scan(step, h0.astype(f32).reshape(b, G, R, P, N),
                              (jnp.moveaxis(states, 1, 0), jnp.moveaxis(chunk_decay, 1, 0)))
    h_prev = jnp.moveaxis(h_prev, 0, 1)
    y_off = jnp.einsum('bcign,bcgrpn->bcigrp', cc, h_prev) * jnp.exp(acum)[..., None]
    return (y_diag + y_off).reshape(b, L, H, P), h_last.reshape(b, H, P, N)


def _select_blocks(q, kmean, n_past):
    nb = kmean.shape[2]
    s = jnp.einsum('bhqd,bhnd->bhqn', q.astype(jnp.float32), kmean)
    valid = jnp.arange(nb)[None, :] < n_past[:, None]
    s = jnp.where(valid, s, -jnp.inf)
    _, idx = lax.top_k(s, MOBA_TOPK)
    slot_valid = jnp.arange(MOBA_TOPK)[None, :] < n_past[:, None]
    return jnp.where(slot_valid, idx, 0), slot_valid


def _moba_attend(q, k_sel, v_sel, slot_valid, k_own, v_own, own_mask):
    scale = ATTN_HEAD_DIM ** -0.5
    s_sel = jnp.einsum('bhqd,bhqkd->bhqk', q, k_sel).astype(jnp.float32) * scale
    s_sel = jnp.where(jnp.repeat(slot_valid, MOBA_BLOCK, axis=-1), s_sel, -jnp.inf)
    s_own = jnp.einsum('bhqd,bhkd->bhqk', q, k_own).astype(jnp.float32) * scale
    s_own = jnp.where(own_mask, s_own, -jnp.inf)
    n_sel = s_sel.shape[-1]
    p = jax.nn.softmax(jnp.concatenate([s_sel, s_own], axis=-1), axis=-1).astype(v_sel.dtype)
    return (jnp.einsum('bhqk,bhqkd->bhqd', p[..., :n_sel], v_sel)
            + jnp.einsum('bhqk,bhkd->bhqd', p[..., n_sel:], v_own))


def _moba_prompt(q, k, v):
    b, S = q.shape[0], q.shape[1]
    q, k, v = q.transpose(0, 2, 1, 3), k.transpose(0, 2, 1, 3), v.transpose(0, 2, 1, 3)
    nb = max(-(-S // MOBA_BLOCK), MOBA_TOPK)
    pad = nb * MOBA_BLOCK - S
    kp = jnp.pad(k, ((0, 0), (0, 0), (0, pad), (0, 0)))
    vp = jnp.pad(v, ((0, 0), (0, 0), (0, pad), (0, 0)))
    kblk = kp.reshape(b, ATTN_HEADS, nb, MOBA_BLOCK, ATTN_HEAD_DIM)
    vblk = vp.reshape(b, ATTN_HEADS, nb, MOBA_BLOCK, ATTN_HEAD_DIM)
    kmean = kblk.astype(jnp.float32).mean(axis=3)
    bi = jnp.arange(b)[:, None, None, None]
    hi = jnp.arange(ATTN_HEADS)[None, :, None, None]

    def chunk(i):
        q0 = i * Q_CHUNK
        qc = lax.dynamic_slice_in_dim(q, q0, Q_CHUNK, axis=2)
        pos = q0 + jnp.arange(Q_CHUNK)
        sel, slot_valid = _select_blocks(qc, kmean, pos // MOBA_BLOCK)
        k_sel = kblk[bi, hi, sel].reshape(b, ATTN_HEADS, Q_CHUNK, MOBA_TOPK * MOBA_BLOCK, ATTN_HEAD_DIM)
        v_sel = vblk[bi, hi, sel].reshape(b, ATTN_HEADS, Q_CHUNK, MOBA_TOPK * MOBA_BLOCK, ATTN_HEAD_DIM)
        own0 = (q0 // MOBA_BLOCK) * MOBA_BLOCK
        k_own = lax.dynamic_slice_in_dim(kp, own0, MOBA_BLOCK, axis=2)
        v_own = lax.dynamic_slice_in_dim(vp, own0, MOBA_BLOCK, axis=2)
        own_mask = (own0 + jnp.arange(MOBA_BLOCK))[None, :] <= pos[:, None]
        return _moba_attend(qc, k_sel, v_sel, slot_valid, k_own, v_own, own_mask)

    out = lax.map(chunk, jnp.arange(S // Q_CHUNK))
    return out.transpose(1, 0, 3, 2, 4).reshape(b, S, ATTN_HEADS * ATTN_HEAD_DIM)


def _moba_sample(q, k, v, cache_k, cache_v, page_table):
    b, L = q.shape[0], q.shape[1]
    q, k, v = q.transpose(0, 2, 1, 3), k.transpose(0, 2, 1, 3), v.transpose(0, 2, 1, 3)
    ppb = MOBA_BLOCK // PAGE_SIZE
    n_full = PAST_LEN // MOBA_BLOCK
    n_tail = PAST_LEN - n_full * MOBA_BLOCK
    kpast = cache_k[page_table[:, :n_full * ppb]]
    kmean = kpast.astype(jnp.float32).reshape(b, n_full, MOBA_BLOCK, ATTN_HEADS, ATTN_HEAD_DIM).mean(axis=2)
    kmean = kmean.transpose(0, 2, 1, 3)
    nb = max(n_full, MOBA_TOPK)
    kmean = jnp.pad(kmean, ((0, 0), (0, 0), (0, nb - n_full), (0, 0)))
    pos = PAST_LEN + jnp.arange(L)
    sel, slot_valid = _select_blocks(q, kmean, jnp.full((L,), n_full, jnp.int32))
    bi = jnp.arange(b)[:, None, None, None, None]
    hi = jnp.arange(ATTN_HEADS)[None, :, None, None, None]
    pages = page_table[bi, sel[..., None] * ppb + jnp.arange(ppb)]
    k_sel = cache_k[pages, :, hi, :].reshape(b, ATTN_HEADS, L, MOBA_TOPK * MOBA_BLOCK, ATTN_HEAD_DIM).astype(q.dtype)
    v_sel = cache_v[pages, :, hi, :].reshape(b, ATTN_HEADS, L, MOBA_TOPK * MOBA_BLOCK, ATTN_HEAD_DIM).astype(q.dtype)
    tail_pt = page_table[:, n_full * ppb:]
    k_tail = cache_k[tail_pt].reshape(b, n_tail, ATTN_HEADS, ATTN_HEAD_DIM).transpose(0, 2, 1, 3)
    v_tail = cache_v[tail_pt].reshape(b, n_tail, ATTN_HEADS, ATTN_HEAD_DIM).transpose(0, 2, 1, 3)
    k_own = jnp.concatenate([k_tail.astype(k.dtype), k], axis=2)
    v_own = jnp.concatenate([v_tail.astype(v.dtype), v], axis=2)
    kpos = n_full * MOBA_BLOCK + jnp.arange(n_tail + L)
    own_mask = kpos[None, :] <= pos[:, None]
    out = _moba_attend(q, k_sel, v_sel, slot_valid, k_own, v_own, own_mask)
    return out.transpose(0, 2, 1, 3).reshape(b, L, ATTN_HEADS * ATTN_HEAD_DIM)


def _layer(x, c, h0, conv0, attn, w_ada, b_ada, norm_in, w_in, conv_w, conv_b,
           dt_bias, a_log, d_skip, ssm_norm, w_branch, w_out):
    b, L, _ = x.shape
    mod = (c @ w_ada + b_ada)[:, None, :]
    shift, scale, gate = jnp.split(mod, 3, axis=-1)
    h = _rmsnorm(x, norm_in) * (1 + scale) + shift
    z_a, xbc, dt_raw, q, k, v, z_b, g_a, g_b = jnp.split(h @ w_in, SPLIT_POINTS, axis=-1)
    xbc_c, conv_new = _causal_conv(xbc, conv0, conv_w, conv_b)
    xs, bm, cm = jnp.split(xbc_c, [D_BRANCH, D_BRANCH + SSM_GROUPS * D_STATE], axis=-1)
    dt = jax.nn.softplus(dt_raw.astype(jnp.float32) + dt_bias.astype(jnp.float32))
    a = -jnp.exp(a_log.astype(jnp.float32))
    xh = xs.reshape(b, L, SSM_HEADS, SSM_HEAD_DIM)
    y_ssd, h_new = _ssd_scan(xh, dt, a, bm.reshape(b, L, SSM_GROUPS, D_STATE),
                             cm.reshape(b, L, SSM_GROUPS, D_STATE), h0)
    y_ssd = (y_ssd.astype(x.dtype) + d_skip[:, None] * xh).reshape(b, L, D_BRANCH)
    y_a = _rmsnorm(y_ssd * jax.nn.silu(z_a), ssm_norm)
    kh = k.reshape(b, L, ATTN_HEADS, ATTN_HEAD_DIM)
    vh = v.reshape(b, L, ATTN_HEADS, ATTN_HEAD_DIM)
    y_b = attn(q.reshape(b, L, ATTN_HEADS, ATTN_HEAD_DIM), kh, vh) * jax.nn.silu(z_b)
    merged = jax.nn.sigmoid(g_a) * (y_a @ w_branch[0]) + jax.nn.sigmoid(g_b) * (y_b @ w_branch[1])
    x_new = x + gate * (merged @ w_out)
    return x_new, kh, vh, h_new.astype(h0.dtype), conv_new


def setup_inputs(seed: int = 0) -> dict:
    key = jax.random.key(seed)
    ks = jax.random.split(key, 24)
    n_pages = PAST_LEN // PAGE_SIZE
    n_used = DEC_BATCH * n_pages
    n_pool = n_used + n_used // 4
    f32 = jnp.float32
    nrm = lambda k, s, sc: jax.random.normal(k, s, f32) * sc
    x_prompt = nrm(ks[0], (BATCH, SEQ, D_MODEL), 1.0)
    x_sample = nrm(ks[1], (DEC_BATCH, DEC_SEQ, D_MODEL), 1.0)
    cache_k = nrm(ks[2], (DEPTH, n_pool, PAGE_SIZE, ATTN_HEADS, ATTN_HEAD_DIM), 1.0)
    cache_v = nrm(ks[3], (DEPTH, n_pool, PAGE_SIZE, ATTN_HEADS, ATTN_HEAD_DIM), 1.0)
    state_ssm = nrm(ks[4], (DEPTH, DEC_BATCH, SSM_HEADS, SSM_HEAD_DIM, D_STATE), 0.1)
    state_conv = nrm(ks[5], (DEPTH, DEC_BATCH, CONV_WIDTH - 1, CONV_DIM), 1.0)
    page_table = jax.random.permutation(ks[6], n_pool)[:n_used].reshape(DEC_BATCH, n_pages).astype(jnp.int32)
    c_prompt = nrm(ks[7], (BATCH, D_MODEL), 1.0)
    c_sample = nrm(ks[8], (DEC_BATCH, D_MODEL), 1.0)
    w_ada = nrm(ks[9], (DEPTH, D_MODEL, 3 * D_MODEL), 0.5 * D_MODEL ** -0.5)
    b_ada = nrm(ks[10], (DEPTH, 3 * D_MODEL), 0.01)
    norm_in = 1.0 + nrm(ks[11], (DEPTH, D_MODEL), 0.05)
    w_in = nrm(ks[12], (DEPTH, D_MODEL, D_IN), D_MODEL ** -0.5)
    conv_w = nrm(ks[13], (DEPTH, CONV_WIDTH, CONV_DIM), CONV_WIDTH ** -0.5)
    conv_b = nrm(ks[14], (DEPTH, CONV_DIM), 0.01)
    u = jax.random.uniform(ks[15], (DEPTH, SSM_HEADS), f32)
    dt0 = jnp.exp(u * (math.log(0.1) - math.log(0.001)) + math.log(0.001))
    dt_bias = dt0 + jnp.log(-jnp.expm1(-dt0))
    a_log = jnp.log(jax.random.uniform(ks[16], (DEPTH, SSM_HEADS), f32, 1.0, 16.0))
    d_skip = 1.0 + nrm(ks[17], (DEPTH, SSM_HEADS), 0.1)
    ssm_norm = 1.0 + nrm(ks[18], (DEPTH, D_BRANCH), 0.05)
    w_branch = nrm(ks[19], (DEPTH, 2, D_BRANCH, D_MODEL), D_BRANCH ** -0.5)
    w_out = nrm(ks[20], (DEPTH, D_MODEL, D_MODEL), D_MODEL ** -0.5)
    norm_final = 1.0 + nrm(ks[21], (D_MODEL,), 0.05)
    return {"x_prompt": x_prompt, "x_sample": x_sample, "cache_k": cache_k, "cache_v": cache_v,
            "state_ssm": state_ssm, "state_conv": state_conv, "page_table": page_table,
            "c_prompt": c_prompt, "c_sample": c_sample, "w_ada": w_ada, "b_ada": b_ada,
            "norm_in": norm_in, "w_in": w_in, "conv_w": conv_w, "conv_b": conv_b,
            "dt_bias": dt_bias, "a_log": a_log, "d_skip": d_skip, "ssm_norm": ssm_norm,
            "w_branch": w_branch, "w_out": w_out, "norm_final": norm_final}


def reference(x_prompt, x_sample, cache_k, cache_v, state_ssm, state_conv, page_table,
              c_prompt, c_sample, w_ada, b_ada, norm_in, w_in, conv_w, conv_b,
              dt_bias, a_log, d_skip, ssm_norm, w_branch, w_out, norm_final):
    xp, xs = x_prompt, x_sample
    kp_l, vp_l, hp_l, cp_l = [], [], [], []
    ks_l, vs_l, hs_l, cs_l = [], [], [], []
    for l in range(DEPTH):
        wl = (w_ada[l], b_ada[l], norm_in[l], w_in[l], conv_w[l], conv_b[l],
              dt_bias[l], a_log[l], d_skip[l], ssm_norm[l], w_branch[l], w_out[l])
        h0 = jnp.zeros((xp.shape[0], SSM_HEADS, SSM_HEAD_DIM, D_STATE), xp.dtype)
        conv0 = jnp.zeros((xp.shape[0], CONV_WIDTH - 1, CONV_DIM), xp.dtype)
        xp, kp, vp, hp, cp = _layer(xp, c_prompt, h0, conv0, _moba_prompt, *wl)
        attn_s = functools.partial(_moba_sample, cache_k=cache_k[l], cache_v=cache_v[l], page_table=page_table)
        xs, ksn, vsn, hsn, csn = _layer(xs, c_sample, state_ssm[l], state_conv[l], attn_s, *wl)
        kp_l.append(kp); vp_l.append(vp); hp_l.append(hp); cp_l.append(cp)
        ks_l.append(ksn); vs_l.append(vsn); hs_l.append(hsn); cs_l.append(csn)
    y_prompt = _rmsnorm(xp, norm_final)
    y_sample = _rmsnorm(xs, norm_final)
    return (y_prompt, y_sample, jnp.stack(kp_l), jnp.stack(vp_l), jnp.stack(hp_l), jnp.stack(cp_l),
            jnp.stack(ks_l), jnp.stack(vs_l), jnp.stack(hs_l), jnp.stack(cs_l))
```

```python
import functools

import jax
import jax.numpy as jnp
from jax import lax
from jax.experimental import pallas as pl
from jax.experimental.pallas import tpu as pltpu

F32 = jnp.float32
BF16 = jnp.bfloat16

NORM_EPS = 1e-6
SSD_CHUNK = 128
MOBA_BLOCK = 256
MOBA_TOPK = 3
CONV_WIDTH = 4
LANES = 128
SUBLANES = 8
HALO = SUBLANES
SAMPLE_ROWS = 16
NEG_BIG = -1e30
VMEM_LIMIT = 56 * 1024 * 1024


def _cparams(sem):
    return pltpu.CompilerParams(dimension_semantics=sem, vmem_limit_bytes=VMEM_LIMIT)


def _resident(shape):
    nd = len(shape)
    return pl.BlockSpec(shape, lambda *_: (0,) * nd, pipeline_mode=pl.Buffered(1))


def _silu(x):
    return x * jax.nn.sigmoid(x)


def _softplus(x):
    return jnp.maximum(x, 0.0) + jnp.log1p(jnp.exp(-jnp.abs(x)))


def _split3(x):
    hi = x.astype(BF16)
    r1 = x - hi.astype(F32)
    mid = r1.astype(BF16)
    lo = (r1 - mid.astype(F32)).astype(BF16)
    return hi, mid, lo


def _dot(a, b):
    return jnp.dot(a, b, preferred_element_type=F32)


def _dot_nt(a, b):
    return lax.dot_general(a, b, (((1,), (1,)), ((), ())), preferred_element_type=F32)


def _dot_exact_rhs(sel, x):
    hi, mid, lo = _split3(x)
    return _dot(sel, hi) + _dot(sel, mid) + _dot(sel, lo)


def _dot_exact_lhs(x, sel):
    hi, mid, lo = _split3(x)
    return _dot(hi, sel) + _dot(mid, sel) + _dot(lo, sel)


def _mod_kernel(c_ref, w_ref, b_ref, o_ref):
    o_ref[...] = _dot_exact_lhs_rhs(c_ref[...], w_ref[...]) + b_ref[...]


def _dot_exact_lhs_rhs(a, b):
    ah = a.astype(BF16)
    al = (a - ah.astype(F32)).astype(BF16)
    bh = b.astype(BF16)
    bl = (b - bh.astype(F32)).astype(BF16)
    return _dot(ah, bh) + _dot(ah, bl) + _dot(al, bh)


def _mod(c, w_ada, b_ada):
    n, d = c.shape
    dn = w_ada.shape[1]
    tn = d
    return pl.pallas_call(
        _mod_kernel,
        grid=(dn // tn,),
        in_specs=[pl.BlockSpec((n, d), lambda j: (0, 0)),
                  pl.BlockSpec((d, tn), lambda j: (0, j)),
                  pl.BlockSpec((1, tn), lambda j: (0, j))],
        out_specs=pl.BlockSpec((n, tn), lambda j: (0, j)),
        out_shape=jax.ShapeDtypeStruct((n, dn), F32),
        compiler_params=_cparams(("arbitrary",)),
        name="mod",
    )(c, w_ada, b_ada)


def _inproj_kernel(x_ref, scale_ref, shift_ref, nin_ref, *refs, n_out):
    w_refs, o_refs = refs[:n_out], refs[n_out:]
    x = x_ref[...]
    y = x * lax.rsqrt(jnp.mean(x * x, axis=-1, keepdims=True) + NORM_EPS) * nin_ref[...]
    h = (y * (1.0 + scale_ref[...]) + shift_ref[...]).astype(BF16)
    for w_ref, o_ref in zip(w_refs, o_refs):
        o_ref[...] = _dot(h, w_ref[...]).astype(o_ref.dtype)


def _mod_specs(tm, rows_per_seq, d, per_row):
    if per_row:
        return pl.BlockSpec((tm, d), lambda i: (i, 0))
    return pl.BlockSpec((None, 1, d), lambda i: ((i * tm) // rows_per_seq, 0, 0))


def _inproj(x, scale, shift, norm_in, weights, out_dtypes, *, tm, rows_per_seq, per_row):
    t, d = x.shape
    n_out = len(weights)
    mspec = _mod_specs(tm, rows_per_seq, d, per_row)
    in_specs = [pl.BlockSpec((tm, d), lambda i: (i, 0)), mspec, mspec, _resident((1, d))]
    in_specs += [_resident(w.shape) for w in weights]
    out_specs = [pl.BlockSpec((tm, w.shape[1]), lambda i: (i, 0)) for w in weights]
    out_shape = [jax.ShapeDtypeStruct((t, w.shape[1]), dt) for w, dt in zip(weights, out_dtypes)]
    return pl.pallas_call(
        functools.partial(_inproj_kernel, n_out=n_out),
        grid=(t // tm,),
        in_specs=in_specs,
        out_specs=out_specs,
        out_shape=out_shape,
        compiler_params=_cparams(("arbitrary",)),
        name="inproj",
    )(x, scale, shift, norm_in, *weights)


def _transpose_rows(x, q):
    if q == LANES:
        return x.T
    xp = jnp.concatenate([x, jnp.zeros((LANES - q, LANES), x.dtype)], axis=0)
    return xp.T[:, :q]


def _ssd_kernel(xs_ref, bcdt_ref, za_ref, cxs_ref, cbc_ref, h0_ref, cwx_ref, cwb_ref, cbx_ref, cbb_ref,
                dtb_ref, alog_ref, dskip_ref, snorm_ref, expand_ref,
                ya_ref, hout_ref, state, halo_xs, halo_bc,
                *, q, n_valid, n_groups, d_state, n_heads, head_dim):
    c = pl.program_id(1)
    gn = n_groups * d_state
    heads_per_group = n_heads // n_groups
    gw = heads_per_group * head_dim

    @pl.when(c == 0)
    def _():
        state[...] = h0_ref[...]
        halo_xs[...] = cxs_ref[...]
        halo_bc[...] = cbc_ref[...]

    xs_raw = xs_ref[...]
    bcdt = bcdt_ref[...]
    bc_raw = bcdt[:, :2 * gn]
    dt_raw = bcdt[:, 2 * gn:2 * gn + LANES]

    def conv(halo, raw, w_ref, b_ref):
        ext = jnp.concatenate([halo[...], raw], axis=0)
        acc = b_ref[...]
        for i in range(CONV_WIDTH):
            sh = CONV_WIDTH - 1 - i
            r = ext if sh == 0 else pltpu.roll(ext, sh, axis=0)
            acc = acc + w_ref[i:i + 1, :] * r[HALO:HALO + q]
        halo[...] = ext[q:q + HALO]
        return _silu(acc)

    xs = conv(halo_xs, xs_raw, cwx_ref, cbx_ref)
    bc = conv(halo_bc, bc_raw, cwb_ref, cbb_ref)

    dt = _softplus(dt_raw + dtb_ref[...])
    if n_valid < q:
        row = lax.broadcasted_iota(jnp.int32, (q, LANES), 0)
        dt = jnp.where(row < n_valid, dt, 0.0)
    a = -jnp.exp(alog_ref[...])
    dta = dt * a

    ri = lax.broadcasted_iota(jnp.int32, (q, q), 0)
    ci = lax.broadcasted_iota(jnp.int32, (q, q), 1)
    causal = ri >= ci
    tril = jnp.where(causal, 1.0, 0.0).astype(BF16)
    acum = _dot_exact_rhs(tril, dta)
    expand = expand_ref[...]
    acum_x = _dot_exact_lhs(acum, expand)
    dt_x = _dot_exact_lhs(dt, expand)
    last_x = acum_x[q - 1:q, :]
    exp_a = jnp.exp(acum_x)
    decay_end = jnp.exp(last_x - acum_x) * dt_x
    xdt = (xs * dt_x).astype(BF16)
    xde = (xs * decay_end).astype(BF16)

    acum_t = _transpose_rows(acum, q)
    lane = lax.broadcasted_iota(jnp.int32, (q, LANES), 1)
    first_half = lane < head_dim

    h_prev = state[...]
    h_prev_b = h_prev.astype(BF16)
    y_parts = []
    s_parts = []
    for g in range(n_groups):
        bm = bc[:, g * d_state:(g + 1) * d_state]
        cm = bc[:, gn + g * d_state:gn + (g + 1) * d_state].astype(BF16)
        cb = _dot_nt(cm, bm.astype(BF16))
        y_off = _dot(cm, h_prev_b[:, g * gw:(g + 1) * gw])
        bm_t = _transpose_rows(bm, q).astype(BF16)
        s_parts.append(_dot(bm_t, xde[:, g * gw:(g + 1) * gw]))
        pair_parts = []
        for pr in range(heads_per_group * head_dim // LANES):
            col0 = g * gw + pr * LANES
            xp = xdt[:, col0:col0 + LANES]
            ypair = None
            for half in range(LANES // head_dim):
                hd = col0 // head_dim + half
                seg = acum[:, hd:hd + 1] - acum_t[hd:hd + 1, :]
                w = (cb * jnp.where(causal, jnp.exp(seg), 0.0)).astype(BF16)
                xh = jnp.where(first_half if half == 0 else jnp.logical_not(first_half), xp, jnp.zeros_like(xp))
                contrib = _dot(w, xh)
                ypair = contrib if ypair is None else ypair + contrib
            pair_parts.append(ypair)
        y_diag = jnp.concatenate(pair_parts, axis=1)
        y_parts.append(y_diag + y_off * exp_a[:, g * gw:(g + 1) * gw])
    y = jnp.concatenate(y_parts, axis=1)
    new_state = h_prev * jnp.exp(last_x) + jnp.concatenate(s_parts, axis=1)
    state[...] = new_state
    hout_ref[...] = new_state

    y = y + dskip_ref[...] * xs
    gated = y * _silu(za_ref[...].astype(F32))
    ya = gated * lax.rsqrt(jnp.mean(gated * gated, axis=-1, keepdims=True) + NORM_EPS) * snorm_ref[...]
    ya_ref[...] = ya.astype(ya_ref.dtype)


def _ssd(xs, bcdt, za, conv0_xs, conv0_bc, h0, cw_xs, cw_bc, cb_xs, cb_bc, dtb, alog, dskip, snorm, expand,
         *, nb, rows_per_seq, q, n_valid, n_groups, d_state, n_heads, head_dim):
    t, dx = xs.shape
    nc = rows_per_seq // q
    wb = bcdt.shape[1]
    gbc = conv0_bc.shape[-1]
    row_spec = lambda w: pl.BlockSpec((q, w), lambda b, c: (b * nc + c, 0))
    seq_spec = lambda r, w: pl.BlockSpec((None, r, w), lambda b, c: (b, 0, 0))
    in_specs = [row_spec(dx), row_spec(wb), row_spec(dx),
                seq_spec(HALO, dx), seq_spec(HALO, gbc), seq_spec(d_state, dx),
                _resident(cw_xs.shape), _resident(cw_bc.shape), _resident(cb_xs.shape), _resident(cb_bc.shape),
                _resident(dtb.shape), _resident(alog.shape), _resident(dskip.shape), _resident(snorm.shape),
                _resident(expand.shape)]
    kern = functools.partial(_ssd_kernel, q=q, n_valid=n_valid, n_groups=n_groups, d_state=d_state,
                             n_heads=n_heads, head_dim=head_dim)
    return pl.pallas_call(
        kern,
        grid=(nb, nc),
        in_specs=in_specs,
        out_specs=[row_spec(dx), seq_spec(d_state, dx)],
        out_shape=[jax.ShapeDtypeStruct((t, dx), BF16), jax.ShapeDtypeStruct((nb, d_state, dx), F32)],
        scratch_shapes=[pltpu.VMEM((d_state, dx), F32), pltpu.VMEM((HALO, dx), F32), pltpu.VMEM((HALO, gbc), F32)],
        compiler_params=_cparams(("arbitrary", "arbitrary")),
        name="ssd",
    )(xs, bcdt, za, conv0_xs, conv0_bc, h0, cw_xs, cw_bc, cb_xs, cb_bc, dtb, alog, dskip, snorm, expand)


def _moba_prompt_kernel(q_ref, k_ref, v_ref, o_ref, kaug, vb, kmean, *, n_blocks, head_dim):
    qb = pl.program_id(2)
    blk = MOBA_BLOCK
    nhalf = LANES // head_dim
    nflag = n_blocks * nhalf

    @pl.when(qb == 0)
    def _():
        kf = k_ref[...]
        s_len = kf.shape[0]
        kaug[:, :LANES] = kf.astype(BF16)
        key_blk = lax.broadcasted_iota(jnp.int32, (s_len, LANES), 0) // blk
        ln = lax.broadcasted_iota(jnp.int32, (s_len, LANES), 1)
        onehot = jnp.logical_and(ln < nflag, (ln % n_blocks) == key_blk)
        kaug[:, LANES:] = jnp.where(onehot, 1.0, 0.0).astype(BF16)
        vb[...] = v_ref[...].astype(BF16)
        km = jnp.sum(kf.reshape(n_blocks, blk, LANES), axis=1) * (1.0 / blk)
        ln2 = lax.broadcasted_iota(jnp.int32, (n_blocks, LANES), 1)
        parts = [jnp.where((ln2 // head_dim) == hf, km, 0.0) for hf in range(nhalf)]
        kmean[...] = jnp.concatenate(parts, axis=0).astype(BF16)

    qv = q_ref[...]
    st = _dot_nt(kmean[...], qv)
    nidx = lax.broadcasted_iota(jnp.int32, (nflag, blk), 0) % n_blocks
    hidx = lax.broadcasted_iota(jnp.int32, (nflag, blk), 0) // n_blocks
    cnt = jnp.zeros((nflag, blk), F32)
    for m in range(n_blocks):
        sm = st[m:m + 1, :]
        for hf in range(1, nhalf):
            sm = jnp.where(hidx == hf, st[hf * n_blocks + m:hf * n_blocks + m + 1, :], sm)
        beats = jnp.logical_or(sm > st, jnp.logical_and(sm == st, m < nidx))
        cnt = cnt + jnp.where(jnp.logical_and(beats, m < qb), 1.0, 0.0)
    keep = jnp.logical_or(jnp.logical_and(nidx < qb, cnt < MOBA_TOPK), nidx == qb)
    bias = jnp.where(keep, 0.0, NEG_BIG)
    bias = jnp.concatenate([bias, jnp.zeros((LANES - nflag, blk), F32)], axis=0)
    flags = bias.T.astype(BF16)

    ln = lax.broadcasted_iota(jnp.int32, (blk, LANES), 1)
    qs = qv * jnp.asarray(head_dim ** -0.5, BF16)
    zero = jnp.zeros_like(qs)
    lhs = []
    for hf in range(nhalf):
        qh = jnp.where((ln // head_dim) == hf, qs, zero)
        fh = jnp.where((ln // n_blocks) == hf, flags, zero)
        lhs.append(jnp.concatenate([qh, fh], axis=1))

    ri = lax.broadcasted_iota(jnp.int32, (blk, blk), 0)
    ci = lax.broadcasted_iota(jnp.int32, (blk, blk), 1)
    causal = ci <= ri

    def block_update(n, carry, diagonal):
        start = pl.multiple_of(n * blk, blk)
        kblk = kaug[pl.ds(start, blk), :]
        vblk = vb[pl.ds(start, blk), :]
        out = []
        for hf in range(nhalf):
            m_old, l_old, acc = carry[hf]
            s = _dot_nt(lhs[hf], kblk)
            if diagonal:
                s = jnp.where(causal, s, NEG_BIG)
            m_new = jnp.maximum(m_old, jnp.max(s, axis=-1, keepdims=True))
            alpha = jnp.exp(m_old - m_new)
            p = jnp.exp(s - m_new)
            l_new = alpha * l_old + jnp.sum(p, axis=-1, keepdims=True)
            acc = alpha * acc + _dot(p.astype(BF16), vblk)
            out.append((m_new, l_new, acc))
        return tuple(out)

    init = tuple((jnp.full((blk, 1), NEG_BIG, F32), jnp.zeros((blk, 1), F32), jnp.zeros((blk, LANES), F32))
                 for _ in range(nhalf))
    carry = lax.fori_loop(0, qb, lambda n, c: block_update(n, c, False), init)
    carry = block_update(qb, carry, True)
    o = jnp.zeros((blk, LANES), F32)
    for hf in range(nhalf):
        _, l_fin, acc = carry[hf]
        o = jnp.where((ln // head_dim) == hf, acc / l_fin, o)
    o_ref[...] = o.astype(o_ref.dtype)


def _moba_prompt(q, k, v, *, nb, seq, head_dim):
    t, d = q.shape
    n_blocks = seq // MOBA_BLOCK
    n_pairs = d // LANES
    kern = functools.partial(_moba_prompt_kernel, n_blocks=n_blocks, head_dim=head_dim)
    nhalf = LANES // head_dim
    return pl.pallas_call(
        kern,
        grid=(nb, n_pairs, n_blocks),
        in_specs=[pl.BlockSpec((MOBA_BLOCK, LANES), lambda b, p, i: (b * n_blocks + i, p)),
                  pl.BlockSpec((seq, LANES), lambda b, p, i: (b, p)),
                  pl.BlockSpec((seq, LANES), lambda b, p, i: (b, p))],
        out_specs=pl.BlockSpec((MOBA_BLOCK, LANES), lambda b, p, i: (b * n_blocks + i, p)),
        out_shape=jax.ShapeDtypeStruct((t, d), BF16),
        scratch_shapes=[pltpu.VMEM((seq, 2 * LANES), BF16), pltpu.VMEM((seq, LANES), BF16),
                        pltpu.VMEM((nhalf * n_blocks, LANES), BF16)],
        compiler_params=_cparams(("arbitrary", "arbitrary", "arbitrary")),
        name="moba_prompt",
    )(q, k, v)


def _block_select_kernel(pt_ref, *refs, pages_per_step, ppb, n_full, n_q):
    page_refs = refs[:pages_per_step]
    q_ref, sel_ref, km = refs[pages_per_step:]
    j = pl.program_id(1)
    blocks_per_step = pages_per_step // ppb
    page_rows = page_refs[0].shape[0]
    for m in range(blocks_per_step):
        tot = None
        for u in range(ppb):
            s = jnp.sum(page_refs[m * ppb + u][...], axis=0)
            tot = s if tot is None else tot + s
        km[j * blocks_per_step + m] = tot * (1.0 / (ppb * page_rows))

    @pl.when(j == pl.num_programs(1) - 1)
    def _():
        kmv = km[...]
        bidx = lax.broadcasted_iota(jnp.int32, (n_full,) + kmv.shape[1:2] + (1,), 0)
        for l in range(n_q):
            s = jnp.sum(kmv * q_ref[l][None], axis=-1, keepdims=True)
            for slot in range(MOBA_TOPK):
                mx = jnp.max(s, axis=0, keepdims=True)
                idx = jnp.min(jnp.where(s == mx, bidx, n_full), axis=0, keepdims=True)
                sel_ref[l, slot] = jnp.broadcast_to(idx[0], sel_ref.shape[2:])
                s = jnp.where(bidx == idx, -jnp.inf, s)


def _block_select(page_table, cache_k, q_sel, *, n_full, ppb):
    nb = page_table.shape[0]
    _, _, page_rows, n_heads, head_dim = cache_k.shape
    n_q = q_sel.shape[1]
    pages_per_step = 8
    n_steps = (n_full * ppb) // pages_per_step
    page_specs = [pl.BlockSpec((None, None, page_rows, n_heads, head_dim),
                               lambda b, j, pt, u=u: (0, pt[b, j * pages_per_step + u], 0, 0, 0))
                  for u in range(pages_per_step)]
    kern = functools.partial(_block_select_kernel, pages_per_step=pages_per_step, ppb=ppb, n_full=n_full, n_q=n_q)
    grid_spec = pltpu.PrefetchScalarGridSpec(
        num_scalar_prefetch=1,
        grid=(nb, n_steps),
        in_specs=page_specs + [pl.BlockSpec((None, n_q, n_heads, head_dim), lambda b, j, pt: (b, 0, 0, 0))],
        out_specs=pl.BlockSpec((None, n_q, MOBA_TOPK, n_heads, LANES), lambda b, j, pt: (b, 0, 0, 0, 0)),
        scratch_shapes=[pltpu.VMEM((n_full, n_heads, head_dim), F32)],
    )
    return pl.pallas_call(
        kern,
        grid_spec=grid_spec,
        out_shape=jax.ShapeDtypeStruct((nb, n_q, MOBA_TOPK, n_heads, LANES), jnp.int32),
        compiler_params=_cparams(("arbitrary", "arbitrary")),
        name="block_select",
    )(page_table, *([cache_k] * pages_per_step), q_sel)


def _moba_sample_kernel(sel_ref, pt_ref, q_ref, kn_ref, vn_ref, k_hbm, v_hbm, o_ref, kbuf, vbuf, sem,
                        *, n_q, ppb, n_heads, head_dim):
    b = pl.program_id(0)
    h = pl.program_id(1)
    step = b * n_heads + h
    n_steps = pl.num_programs(0) * n_heads
    n_sel = n_q * MOBA_TOPK
    n_slab = n_sel * ppb
    page_rows = kbuf.shape[2]

    def copies(st, slot):
        bb = st // n_heads
        hh = st % n_heads
        out = []
        for i in range(n_slab):
            blk_id = sel_ref[st * n_sel + i // ppb]
            page = pt_ref[bb, blk_id * ppb + (i % ppb)]
            out.append(pltpu.make_async_copy(k_hbm.at[0, page, :, hh, :], kbuf.at[slot, i], sem.at[0, slot]))
            out.append(pltpu.make_async_copy(v_hbm.at[0, page, :, hh, :], vbuf.at[slot, i], sem.at[1, slot]))
        return out

    @pl.when(step == 0)
    def _():
        for cp in copies(0, 0):
            cp.start()

    @pl.when(step + 1 < n_steps)
    def _():
        for cp in copies(step + 1, (step + 1) % 2):
            cp.start()

    slot = step % 2
    for cp in copies(step, slot):
        cp.wait()

    rows = q_ref.shape[0]
    qv = (q_ref[...].astype(F32) * (head_dim ** -0.5)).astype(BF16)
    kn = kn_ref[...].astype(BF16)
    vn = vn_ref[...].astype(BF16)
    ri = lax.broadcasted_iota(jnp.int32, (rows, rows), 0)
    ci = lax.broadcasted_iota(jnp.int32, (rows, rows), 1)
    s_own = jnp.where(ci <= ri, _dot_nt(qv, kn), NEG_BIG)
    keys_per_q = MOBA_TOPK * ppb * page_rows
    rsel = lax.broadcasted_iota(jnp.int32, (rows, keys_per_q), 0)
    s_sel = []
    m = jnp.max(s_own, axis=-1, keepdims=True)
    for l in range(n_q):
        kl = kbuf[slot, l * MOBA_TOPK * ppb:(l + 1) * MOBA_TOPK * ppb].reshape(keys_per_q, head_dim).astype(BF16)
        s = jnp.where(rsel == l, _dot_nt(qv, kl), NEG_BIG)
        s_sel.append(s)
        m = jnp.maximum(m, jnp.max(s, axis=-1, keepdims=True))
    p_own = jnp.exp(s_own - m)
    den = jnp.sum(p_own, axis=-1, keepdims=True)
    acc = _dot(p_own.astype(BF16), vn)
    for l in range(n_q):
        vl = vbuf[slot, l * MOBA_TOPK * ppb:(l + 1) * MOBA_TOPK * ppb].reshape(keys_per_q, head_dim).astype(BF16)
        p = jnp.exp(s_sel[l] - m)
        den = den + jnp.sum(p, axis=-1, keepdims=True)
        acc = acc + _dot(p.astype(BF16), vl)
    o_ref[...] = (acc / den).astype(o_ref.dtype)


def _moba_sample(sel_flat, page_table, q_h, k_h, v_h, cache_k, cache_v, *, n_q, ppb):
    nb, n_heads, rows, head_dim = q_h.shape
    page_rows = cache_k.shape[2]
    n_slab = n_q * MOBA_TOPK * ppb
    kern = functools.partial(_moba_sample_kernel, n_q=n_q, ppb=ppb, n_heads=n_heads, head_dim=head_dim)
    tok_spec = pl.BlockSpec((None, None, rows, head_dim), lambda b, h, sel, pt: (b, h, 0, 0))
    grid_spec = pltpu.PrefetchScalarGridSpec(
        num_scalar_prefetch=2,
        grid=(nb, n_heads),
        in_specs=[tok_spec, tok_spec, tok_spec,
                  pl.BlockSpec(memory_space=pl.ANY), pl.BlockSpec(memory_space=pl.ANY)],
        out_specs=tok_spec,
        scratch_shapes=[pltpu.VMEM((2, n_slab, page_rows, head_dim), F32),
                        pltpu.VMEM((2, n_slab, page_rows, head_dim), F32),
                        pltpu.SemaphoreType.DMA((2, 2))],
    )
    return pl.pallas_call(
        kern,
        grid_spec=grid_spec,
        out_shape=jax.ShapeDtypeStruct((nb, n_heads, rows, head_dim), BF16),
        compiler_params=_cparams(("arbitrary", "arbitrary")),
        name="moba_sample",
    )(sel_flat, page_table, q_h, k_h, v_h, cache_k, cache_v)


def _merge_kernel(x_ref, gate_ref, ya_ref, yb_ref, zb_ref, ga_ref, gb_ref, wa_ref, wb_ref, wo_ref, nf_ref, o_ref):
    yb = (yb_ref[...].astype(F32) * _silu(zb_ref[...].astype(F32))).astype(BF16)
    ma = _dot(ya_ref[...], wa_ref[...])
    mb = _dot(yb, wb_ref[...])
    merged = jax.nn.sigmoid(ga_ref[...].astype(F32)) * ma + jax.nn.sigmoid(gb_ref[...].astype(F32)) * mb
    xn = x_ref[...] + gate_ref[...] * _dot(merged.astype(BF16), wo_ref[...])
    y = xn * lax.rsqrt(jnp.mean(xn * xn, axis=-1, keepdims=True) + NORM_EPS) * nf_ref[...]
    o_ref[...] = y


def _merge(x, gate, ya, yb, zb, ga, gb, wa, wb, wo, norm_final, *, tm, rows_per_seq, per_row):
    t, d = x.shape
    row = pl.BlockSpec((tm, d), lambda i: (i, 0))
    return pl.pallas_call(
        _merge_kernel,
        grid=(t // tm,),
        in_specs=[row, _mod_specs(tm, rows_per_seq, d, per_row), row, row, row, row, row,
                  _resident(wa.shape), _resident(wb.shape), _resident(wo.shape), _resident(norm_final.shape)],
        out_specs=row,
        out_shape=jax.ShapeDtypeStruct((t, d), F32),
        compiler_params=_cparams(("arbitrary",)),
        name="merge",
    )(x, gate, ya, yb, zb, ga, gb, wa, wb, wo, norm_final)


def kernel(x_prompt, x_sample, cache_k, cache_v, state_ssm, state_conv, page_table, c_prompt, c_sample, w_ada, b_ada,
           norm_in, w_in, conv_w, conv_b, dt_bias, a_log, d_skip, ssm_norm, w_branch, w_out, norm_final):
    depth = w_in.shape[0]
    assert depth == 1, "single-layer step"
    nbp, seq, d = x_prompt.shape
    nbs, n_q, _ = x_sample.shape
    _, _, n_heads, head_dim, d_state = state_ssm.shape
    conv_dim = conv_w.shape[-1]
    dx = n_heads * head_dim
    gbc = conv_dim - dx
    n_groups = gbc // (2 * d_state)
    page_rows = cache_k.shape[2]
    ppb = MOBA_BLOCK // page_rows
    past_len = page_table.shape[1] * page_rows
    n_full = past_len // MOBA_BLOCK
    assert past_len == n_full * MOBA_BLOCK and n_full >= MOBA_TOPK
    assert seq % MOBA_BLOCK == 0 and seq // MOBA_BLOCK >= MOBA_TOPK and seq % SSD_CHUNK == 0
    assert cache_k.shape[3] == n_heads and cache_k.shape[4] == head_dim and n_q <= SAMPLE_ROWS

    w = w_in[0]
    o_xbc = dx
    o_dt = o_xbc + conv_dim
    o_q = o_dt + n_heads
    cols = lambda a, n: w[:, a:a + n].astype(BF16)
    w_za = cols(0, dx)
    w_xs = cols(o_xbc, dx)
    w_bcdt = jnp.concatenate([cols(o_xbc + dx, gbc + n_heads), jnp.zeros((d, LANES - n_heads), BF16)], axis=1)
    w_q, w_k, w_v, w_zb = (cols(o_q + i * dx, dx) for i in range(4))
    w_ga = cols(o_q + 4 * dx, d)
    w_gb = cols(o_q + 4 * dx + d, d)
    weights = [w_za, w_xs, w_bcdt, w_q, w_k, w_v, w_zb, w_ga, w_gb]
    out_dtypes = [BF16, F32, F32, BF16, F32, F32, BF16, BF16, BF16]

    pad_lanes = lambda v: jnp.concatenate([v, jnp.zeros((1, LANES - v.shape[1]), F32)], axis=1)
    dtb = pad_lanes(dt_bias)
    alog = pad_lanes(a_log)
    dskip_x = jnp.repeat(d_skip[0], head_dim)[None, :]
    head_of_col = jnp.arange(dx, dtype=jnp.int32) // head_dim
    expand = (jnp.arange(LANES, dtype=jnp.int32)[:, None] == head_of_col[None, :]).astype(BF16)
    cw_xs, cw_bc = conv_w[0, :, :dx], conv_w[0, :, dx:]
    cb_xs, cb_bc = conv_b[:, :dx], conv_b[:, dx:]
    wa, wb = w_branch[0, 0].astype(BF16), w_branch[0, 1].astype(BF16)
    wo = w_out[0].astype(BF16)
    nf = norm_final[None, :]

    mod = _mod(jnp.concatenate([c_prompt, c_sample], axis=0), w_ada[0], b_ada)
    shift, scale, gate = mod[:, :d], mod[:, d:2 * d], mod[:, 2 * d:]

    ssd_static = dict(n_groups=n_groups, d_state=d_state, n_heads=n_heads, head_dim=head_dim)

    tp = nbp * seq
    xp = x_prompt.reshape(tp, d)
    seq_mod = lambda m: m[:nbp, None, :]
    za, xs, bcdt, q, k, v, zb, ga, gb = _inproj(xp, seq_mod(scale), seq_mod(shift), norm_in, weights, out_dtypes,
                                                tm=512, rows_per_seq=seq, per_row=False)
    ya, h_p = _ssd(xs, bcdt, za, jnp.zeros((nbp, HALO, dx), F32), jnp.zeros((nbp, HALO, gbc), F32),
                   jnp.zeros((nbp, d_state, dx), F32), cw_xs, cw_bc, cb_xs, cb_bc, dtb, alog, dskip_x, ssm_norm, expand,
                   nb=nbp, rows_per_seq=seq, q=SSD_CHUNK, n_valid=SSD_CHUNK, **ssd_static)
    yb = _moba_prompt(q, k, v, nb=nbp, seq=seq, head_dim=head_dim)
    y_prompt = _merge(xp, seq_mod(gate), ya, yb, zb, ga, gb, wa, wb, wo, nf,
                      tm=512, rows_per_seq=seq, per_row=False).reshape(nbp, seq, d)
    k_prompt = k.reshape(1, nbp, seq, n_heads, head_dim)
    v_prompt = v.reshape(1, nbp, seq, n_heads, head_dim)
    ssm_prompt = h_p.reshape(nbp, d_state, n_heads, head_dim).transpose(0, 2, 3, 1)[None]
    keep = CONV_WIDTH - 1
    conv_prompt = jnp.concatenate([xs.reshape(nbp, seq, dx)[:, seq - keep:],
                                   bcdt.reshape(nbp, seq, -1)[:, seq - keep:, :gbc]], axis=-1)[None]

    rs = SAMPLE_ROWS
    ts = nbs * rs
    xsm = jnp.pad(x_sample, ((0, 0), (0, rs - n_q), (0, 0))).reshape(ts, d)
    row_mod = lambda m: jnp.repeat(m[nbp:], rs, axis=0)
    za_s, xs_s, bcdt_s, q_s, k_s, v_s, zb_s, ga_s, gb_s = _inproj(
        xsm, row_mod(scale), row_mod(shift), norm_in, weights, out_dtypes, tm=ts, rows_per_seq=rs, per_row=True)
    halo_pad = lambda a: jnp.pad(a, ((0, 0), (HALO - keep, 0), (0, 0)))
    h0_s = state_ssm[0].transpose(0, 3, 1, 2).reshape(nbs, d_state, dx)
    ya_s, h_s = _ssd(xs_s, bcdt_s, za_s, halo_pad(state_conv[0, :, :, :dx]), halo_pad(state_conv[0, :, :, dx:]), h0_s,
                     cw_xs, cw_bc, cb_xs, cb_bc, dtb, alog, dskip_x, ssm_norm, expand,
                     nb=nbs, rows_per_seq=rs, q=rs, n_valid=n_q, **ssd_static)
    heads = lambda a: a.reshape(nbs, rs, n_heads, head_dim).transpose(0, 2, 1, 3)
    q_sel = q_s.reshape(nbs, rs, n_heads, head_dim)[:, :n_q].astype(F32)
    sel = _block_select(page_table, cache_k, q_sel, n_full=n_full, ppb=ppb)[..., 0]
    sel_flat = sel.transpose(0, 3, 1, 2).reshape(-1)
    yb_h = _moba_sample(sel_flat, page_table, heads(q_s), heads(k_s), heads(v_s), cache_k, cache_v, n_q=n_q, ppb=ppb)
    yb_s = yb_h.transpose(0, 2, 1, 3).reshape(ts, dx)
    y_s = _merge(xsm, row_mod(gate), ya_s, yb_s, zb_s, ga_s, gb_s, wa, wb, wo, nf, tm=ts, rows_per_seq=rs, per_row=True)
    y_sample = y_s.reshape(nbs, rs, d)[:, :n_q]
    tok = lambda a: a.reshape(nbs, rs, n_heads, head_dim)[:, :n_q][None]
    k_sample, v_sample = tok(k_s), tok(v_s)
    ssm_sample = h_s.reshape(nbs, d_state, n_heads, head_dim).transpose(0, 2, 3, 1)[None]
    xbc_s = jnp.concatenate([xs_s.reshape(nbs, rs, dx), bcdt_s.reshape(nbs, rs, -1)[:, :, :gbc]], axis=-1)
    conv_all = jnp.concatenate([state_conv[0], xbc_s[:, :n_q]], axis=1)
    conv_sample = conv_all[:, n_q:][None]

    return (y_prompt, y_sample, k_prompt, v_prompt, ssm_prompt, conv_prompt,
            k_sample, v_sample, ssm_sample, conv_sample)
```
